```python
import jax
import jax.numpy as jnp
from jax import lax
import numpy as np

D_MODEL = 1024
BATCH = 8
SEQ = 2048
DEPTH = 4
DEC_BATCH = 32
DEC_SEQ = 4
PAST_LEN = 8192
PAGE_SIZE = 128

D_MIX = D_MODEL
NSA_WIDTH = D_MIX // 2
GM_WIDTH = D_MIX - NSA_WIDTH
HEAD_DIM = 64
N_HEADS = NSA_WIDTH // HEAD_DIM
N_KV = 2
HPG = N_HEADS // N_KV
KV_WIDTH = N_KV * HEAD_DIM
L_BLK = 64
N_SEL = 16
WINDOW = 512
WIN_QBLK = 128
SEL_QBLK = 16
CMP_HIDDEN = 2 * HEAD_DIM
GM_DIM = 64
GM_GROUPS = GM_WIDTH // GM_DIM
CHUNK = 128
N_EGROUPS = 4
EXPERTS_PER_GROUP = 4
N_EXPERTS = N_EGROUPS * EXPERTS_PER_GROUP
TOP_K = 2
EXPERT_HIDDEN = 512
N_PROJ = NSA_WIDTH + 6 * KV_WIDTH + 3 * N_HEADS + 2 * GM_WIDTH
EPS = 1e-6
NEG_INF = -1e30

kernel_name = 'hymba_gmlp_nsa_hmoe_step'


def _rms(x, g):
    xf = x.astype(jnp.float32)
    y = xf * lax.rsqrt(jnp.mean(xf * xf, axis=-1, keepdims=True) + EPS)
    return (y * g.astype(jnp.float32)).astype(x.dtype)


def _alibi_slopes():
    h = jnp.arange(1, N_HEADS + 1, dtype=jnp.float32)
    return jnp.exp2(-8.0 * h / N_HEADS).reshape(N_KV, HPG)


def _masked_softmax(s, mask):
    p = jax.nn.softmax(jnp.where(mask, s, NEG_INF), axis=-1)
    return jnp.where(mask, p, 0.0)


def _split_proj(p):
    sizes = [NSA_WIDTH, 2 * KV_WIDTH, 2 * KV_WIDTH, 2 * KV_WIDTH, 3 * N_HEADS, GM_WIDTH, GM_WIDTH]
    cuts = [int(c) for c in np.cumsum(sizes)[:-1]]
    q, kvc, kvs, kvw, gt, u, v = jnp.split(p, cuts, axis=-1)
    lead = p.shape[:-1]
    q = q.reshape(*lead, N_KV, HPG, HEAD_DIM)
    kvc = kvc.reshape(*lead, 2, N_KV, HEAD_DIM)
    kvs = kvs.reshape(*lead, 2, N_KV, HEAD_DIM)
    kvw = kvw.reshape(*lead, 2, N_KV, HEAD_DIM)
    gates = jax.nn.sigmoid(gt.astype(jnp.float32)).reshape(*lead, 3, N_KV, HPG)
    return q, kvc, kvs, kvw, gates, u, v


def _compress(blocks, pe, w1, w2):
    B, NB = blocks.shape[:2]
    x = blocks + jnp.transpose(pe, (1, 0, 2))[:, :, None, :]
    x = jnp.transpose(x, (0, 1, 3, 4, 2, 5)).reshape(B, NB, 2, N_KV, L_BLK * HEAD_DIM)
    h = jax.nn.gelu(jnp.einsum('bnckf,cfh->bnckh', x, w1))
    return jnp.einsum('bnckh,che->bncke', h, w2)


def _cmp_attention(qg, t, kc, vc, key_pos, slopes):
    s = jnp.einsum('btghd,bngd->bghtn', qg, kc).astype(jnp.float32) * HEAD_DIM ** -0.5
    dist = t[:, None] - key_pos[None, :]
    s = s - slopes[None, :, :, None, None] * dist.astype(jnp.float32)
    p = _masked_softmax(s, dist >= 0)
    o = jnp.einsum('bghtn,bngd->btghd', p, vc.astype(jnp.float32))
    return o, jnp.sum(p, axis=2)


def _select_blocks(imp, t, n_blocks):
    cand = jnp.arange(n_blocks)[None, :] < (t // L_BLK)[:, None]
    score = jnp.where(cand, imp, -1.0)
    val, idx = lax.top_k(score, min(N_SEL - 1, n_blocks))
    return jnp.swapaxes(idx, 1, 2).astype(jnp.int32), jnp.swapaxes(val, 1, 2) >= 0.0


def _sel_core(qg, t, ks, vs, pos_sel, valid, kc, vc, pos_cur, slopes):
    B, Q, G, K, L = ks.shape[:5]
    scale = HEAD_DIM ** -0.5
    sl = slopes[None, None, :, :, None]
    s_sel = jnp.einsum('bqghd,bqgkld->bqghkl', qg, ks).astype(jnp.float32).reshape(B, Q, G, HPG, K * L) * scale
    d_sel = (t[None, :, None, None, None] - pos_sel).reshape(B, Q, G, 1, K * L)
    s_sel = s_sel - sl * d_sel.astype(jnp.float32)
    m_sel = jnp.broadcast_to(jnp.repeat(valid, L, axis=-1)[:, :, :, None, :], s_sel.shape)
    s_cur = jnp.einsum('bqghd,blgd->bqghl', qg, kc).astype(jnp.float32) * scale
    d_cur = t[:, None] - pos_cur[None, :]
    s_cur = s_cur - sl * d_cur[None, :, None, None, :].astype(jnp.float32)
    m_cur = jnp.broadcast_to((d_cur >= 0)[None, :, None, None, :], s_cur.shape)
    p = _masked_softmax(jnp.concatenate([s_sel, s_cur], -1), jnp.concatenate([m_sel, m_cur], -1))
    p_sel = p[..., :K * L].reshape(B, Q, G, HPG, K, L)
    p_cur = p[..., K * L:]
    return (jnp.einsum('bqghkl,bqgkld->bqghd', p_sel, vs.astype(jnp.float32))
            + jnp.einsum('bqghl,blgd->bqghd', p_cur, vc.astype(jnp.float32)))


def _win_core(qg, t, kw, vw, pos_k, slopes):
    s = jnp.einsum('bqghd,bkgd->bqghk', qg, kw).astype(jnp.float32) * HEAD_DIM ** -0.5
    d = t[:, None] - pos_k[None, :]
    mask = (d >= 0) & (d <= WINDOW) & (pos_k >= 0)[None, :]
    s = s - slopes[None, None, :, :, None] * d[None, :, None, None, :].astype(jnp.float32)
    p = _masked_softmax(s, mask[None, :, None, None, :])
    return jnp.einsum('bqghk,bkgd->bqghd', p, vw.astype(jnp.float32))


def _merge_branches(gates, o_cmp, o_sel, o_win):
    g = gates[..., None]
    return g[:, :, 0] * o_cmp + g[:, :, 1] * o_sel + g[:, :, 2] * o_win


def _nsa_prompt(q, kv_cmp, kv_sel, kv_win, gates, pe_cmp, w_cmp1, w_cmp2, slopes):
    B, T = q.shape[:2]
    nb = T // L_BLK
    t = jnp.arange(T, dtype=jnp.int32)
    off = jnp.arange(L_BLK, dtype=jnp.int32)
    comp = _compress(kv_cmp.reshape(B, nb, L_BLK, 2, N_KV, HEAD_DIM), pe_cmp, w_cmp1, w_cmp2)
    blk_end = (jnp.arange(nb, dtype=jnp.int32) + 1) * L_BLK - 1
    o_cmp, imp = _cmp_attention(q, t, comp[:, :, 0], comp[:, :, 1], blk_end, slopes)
    idx, valid = _select_blocks(imp, t, nb)
    kb = kv_sel[:, :, 0].reshape(B, nb, L_BLK, N_KV, HEAD_DIM)
    vb = kv_sel[:, :, 1].reshape(B, nb, L_BLK, N_KV, HEAD_DIM)
    nq = T // SEL_QBLK

    def qblocks(a):
        return jnp.moveaxis(a.reshape(B, nq, SEL_QBLK, *a.shape[2:]), 1, 0)

    b_ix = jnp.arange(B)[:, None, None, None]
    g_ix = jnp.arange(N_KV)[None, None, :, None]

    def sel_step(args):
        i, qi, idx_i, val_i = args
        c = (i * SEL_QBLK) // L_BLK
        ti = i * SEL_QBLK + jnp.arange(SEL_QBLK, dtype=jnp.int32)
        ks = kb[b_ix, idx_i, :, g_ix]
        vs = vb[b_ix, idx_i, :, g_ix]
        kc = lax.dynamic_index_in_dim(kb, c, axis=1, keepdims=False)
        vc = lax.dynamic_index_in_dim(vb, c, axis=1, keepdims=False)
        return _sel_core(qi, ti, ks, vs, idx_i[..., None] * L_BLK + off, val_i, kc, vc, c * L_BLK + off, slopes)

    o_sel = lax.map(sel_step, (jnp.arange(nq, dtype=jnp.int32), qblocks(q), qblocks(idx), qblocks(valid)))
    o_sel = jnp.moveaxis(o_sel, 0, 1).reshape(B, T, N_KV, HPG, HEAD_DIM)
    pad = [(0, 0), (WINDOW, 0), (0, 0), (0, 0)]
    kwp = jnp.pad(kv_win[:, :, 0], pad)
    vwp = jnp.pad(kv_win[:, :, 1], pad)
    nw = T // WIN_QBLK
    span = WINDOW + WIN_QBLK

    def win_step(args):
        i, qi = args
        start = i * WIN_QBLK
        ki = lax.dynamic_slice_in_dim(kwp, start, span, axis=1)
        vi = lax.dynamic_slice_in_dim(vwp, start, span, axis=1)
        ti = start + jnp.arange(WIN_QBLK, dtype=jnp.int32)
        pos = start - WINDOW + jnp.arange(span, dtype=jnp.int32)
        return _win_core(qi, ti, ki, vi, pos, slopes)

    qw = jnp.moveaxis(q.reshape(B, nw, WIN_QBLK, N_KV, HPG, HEAD_DIM), 1, 0)
    o_win = lax.map(win_step, (jnp.arange(nw, dtype=jnp.int32), qw))
    o_win = jnp.moveaxis(o_win, 0, 1).reshape(B, T, N_KV, HPG, HEAD_DIM)
    return _merge_branches(gates, o_cmp, o_sel, o_win).astype(q.dtype).reshape(B, T, NSA_WIDTH)


def _nsa_sample(q, kv_cmp, kv_sel, kv_win, gates, cache_cmp, cache_sel, win_buf, page_table,
                pe_cmp, w_cmp1, w_cmp2, slopes):
    Bd, S = q.shape[:2]
    past = page_table.shape[1] * PAGE_SIZE
    nb = past // L_BLK
    t = past + jnp.arange(S, dtype=jnp.int32)
    off = jnp.arange(L_BLK, dtype=jnp.int32)
    past_cmp = cache_cmp[page_table].reshape(Bd, nb, L_BLK, 2, N_KV, HEAD_DIM)
    comp = _compress(past_cmp, pe_cmp, w_cmp1, w_cmp2)
    blk_end = (jnp.arange(nb, dtype=jnp.int32) + 1) * L_BLK - 1
    o_cmp, imp = _cmp_attention(q, t, comp[:, :, 0], comp[:, :, 1], blk_end, slopes)
    idx, valid = _select_blocks(imp, t, nb)
    bpp = PAGE_SIZE // L_BLK
    b_ix = jnp.arange(Bd)[:, None, None, None]
    g_ix = jnp.arange(N_KV)[None, None, :, None]
    phys = page_table[b_ix, idx // bpp] * bpp + idx % bpp
    pool_blocks = cache_sel.reshape(-1, L_BLK, 2, N_KV, HEAD_DIM)
    rows = pool_blocks[phys, :, :, g_ix]
    o_sel = _sel_core(q, t, rows[..., 0, :], rows[..., 1, :], idx[..., None] * L_BLK + off, valid,
                      kv_sel[:, :, 0], kv_sel[:, :, 1], t, slopes)
    wb = win_buf.shape[1]
    kw = jnp.concatenate([win_buf[:, :, 0], kv_win[:, :, 0]], axis=1)
    vw = jnp.concatenate([win_buf[:, :, 1], kv_win[:, :, 1]], axis=1)
    pos = past - wb + jnp.arange(wb + S, dtype=jnp.int32)
    o_win = _win_core(q, t, kw, vw, pos, slopes)
    return _merge_branches(gates, o_cmp, o_sel, o_win).astype(q.dtype).reshape(Bd, S, NSA_WIDTH)


def _gmlp(u_raw, v_raw, ln_g, ln_b, w_s, b_s):
    B, T = u_raw.shape[:2]
    u = jax.nn.gelu(u_raw).reshape(B, T, GM_GROUPS, GM_DIM)
    vf = jax.nn.gelu(v_raw.astype(jnp.float32)).reshape(B, T, GM_GROUPS, GM_DIM)
    mu = jnp.mean(vf, axis=-1, keepdims=True)
    var = jnp.mean(jnp.square(vf - mu), axis=-1, keepdims=True)
    vn = ((vf - mu) * lax.rsqrt(var + EPS) * ln_g + ln_b).astype(u_raw.dtype)
    n = min(T, CHUNK)
    vc = vn.reshape(B, T // n, n, GM_GROUPS, GM_DIM)
    ws = jnp.tril(w_s[:, :n, :n])
    mixed = jnp.einsum('gts,bcsgd->bctgd', ws, vc) + jnp.transpose(b_s[:, :n])[None, None, :, :, None]
    out = u * mixed.reshape(B, T, GM_GROUPS, GM_DIM)
    return out.reshape(B, T, GM_WIDTH), vn


def _mix_out(o_nsa, o_gm, g, w_out):
    o = jnp.concatenate([_rms(o_nsa, g[:NSA_WIDTH]), _rms(o_gm, g[NSA_WIDTH:])], axis=-1)
    return o @ w_out


def _hier_moe(x, w_rg, b_rg, w_re, b_re, w_gate, w_up, w_down):
    lead = x.shape[:-1]
    xt = x.reshape(-1, D_MODEL)
    n = xt.shape[0]
    rows = jnp.arange(n)
    lg = (xt @ w_rg).astype(jnp.float32) + b_rg
    grp = jnp.argmax(lg, axis=-1)
    g_gate = jax.nn.softmax(lg, axis=-1)[rows, grp][:, None]
    le = ((xt @ w_re).astype(jnp.float32) + b_re).reshape(n, N_EGROUPS, EXPERTS_PER_GROUP)[rows, grp]
    top_p, top_i = lax.top_k(jax.nn.softmax(le, axis=-1), TOP_K)
    w = g_gate * top_p / jnp.sum(top_p, axis=-1, keepdims=True)
    eid = grp[:, None] * EXPERTS_PER_GROUP + top_i
    combine = jnp.zeros((n, N_EXPERTS), jnp.float32).at[rows[:, None], eid].add(w)
    y = jnp.zeros((n, D_MODEL), jnp.float32)
    for e in range(N_EXPERTS):
        h = jax.nn.silu(xt @ w_gate[e]) * (xt @ w_up[e])
        y = y + combine[:, e:e + 1] * (h @ w_down[e]).astype(jnp.float32)
    return y.astype(x.dtype).reshape(*lead, D_MODEL)


def setup_inputs(seed: int = 0) -> dict:
    key = jax.random.key(seed)
    ks = iter(jax.random.split(key, 32))

    def nrm(shape, scale):
        return jax.random.normal(next(ks), shape, jnp.float32) * scale

    n_pages = PAST_LEN // PAGE_SIZE
    n_used = DEC_BATCH * n_pages
    n_pool = n_used + max(1, n_used // 4)
    wbuf = min(WINDOW, PAST_LEN)
    pool_shape = (DEPTH, n_pool, PAGE_SIZE, 2, N_KV, HEAD_DIM)
    x_prompt = nrm((BATCH, SEQ, D_MODEL), 1.0)
    x_sample = nrm((DEC_BATCH, DEC_SEQ, D_MODEL), 1.0)
    cache_cmp_kv = nrm(pool_shape, 1.0)
    cache_sel_kv = nrm(pool_shape, 1.0)
    state_win_kv = nrm((DEPTH, DEC_BATCH, wbuf, 2, N_KV, HEAD_DIM), 1.0)
    perm = jax.random.permutation(next(ks), n_pool)[:n_used].astype(jnp.int32)
    page_table = perm.reshape(DEC_BATCH, n_pages)
    return {
        'x_prompt': x_prompt,
        'x_sample': x_sample,
        'cache_cmp_kv': cache_cmp_kv,
        'cache_sel_kv': cache_sel_kv,
        'state_win_kv': state_win_kv,
        'page_table': page_table,
        'norm1_g': 1.0 + nrm((DEPTH, D_MODEL), 0.02),
        'w_in': nrm((DEPTH, D_MODEL, N_PROJ), D_MODEL ** -0.5),
        'pe_cmp': nrm((DEPTH, 2, L_BLK, HEAD_DIM), 0.5),
        'w_cmp1': nrm((DEPTH, 2, L_BLK * HEAD_DIM, CMP_HIDDEN), (L_BLK * HEAD_DIM) ** -0.5),
        'w_cmp2': nrm((DEPTH, 2, CMP_HIDDEN, HEAD_DIM), CMP_HIDDEN ** -0.5),
        'gm_ln_g': 1.0 + nrm((DEPTH, GM_GROUPS, GM_DIM), 0.02),
        'gm_ln_b': nrm((DEPTH, GM_GROUPS, GM_DIM), 0.02),
        'w_spatial': nrm((DEPTH, GM_GROUPS, CHUNK, CHUNK), 0.5 * CHUNK ** -0.5),
        'b_spatial': 1.0 + nrm((DEPTH, GM_GROUPS, CHUNK), 0.1),
        'out_norm_g': 1.0 + nrm((DEPTH, D_MIX), 0.02),
        'w_out': nrm((DEPTH, D_MIX, D_MODEL), D_MIX ** -0.5),
        'norm2_g': 1.0 + nrm((DEPTH, D_MODEL), 0.02),
        'w_router_group': nrm((DEPTH, D_MODEL, N_EGROUPS), D_MODEL ** -0.5),
        'b_router_group': nrm((DEPTH, N_EGROUPS), 0.01),
        'w_router_expert': nrm((DEPTH, D_MODEL, N_EXPERTS), D_MODEL ** -0.5),
        'b_router_expert': nrm((DEPTH, N_EXPERTS), 0.01),
        'w_gate_e': nrm((DEPTH, N_EXPERTS, D_MODEL, EXPERT_HIDDEN), D_MODEL ** -0.5),
        'w_up_e': nrm((DEPTH, N_EXPERTS, D_MODEL, EXPERT_HIDDEN), D_MODEL ** -0.5),
        'w_down_e': nrm((DEPTH, N_EXPERTS, EXPERT_HIDDEN, D_MODEL), EXPERT_HIDDEN ** -0.5),
        'final_norm_g': 1.0 + nrm((D_MODEL,), 0.02),
    }


def reference(x_prompt, x_sample, cache_cmp_kv, cache_sel_kv, state_win_kv, page_table,
              norm1_g, w_in, pe_cmp, w_cmp1, w_cmp2, gm_ln_g, gm_ln_b, w_spatial, b_spatial,
              out_norm_g, w_out, norm2_g, w_router_group, b_router_group, w_router_expert,
              b_router_expert, w_gate_e, w_up_e, w_down_e, final_norm_g):
    slopes = _alibi_slopes()
    xp = x_prompt
    xs = x_sample
    cmp_p, cmp_s, sel_p, sel_s, win_p, win_s, gm_s = [], [], [], [], [], [], []
    for l in range(DEPTH):
        q, kvc, kvs, kvw, gates, u, v = _split_proj(_rms(xp, norm1_g[l]) @ w_in[l])
        o_nsa = _nsa_prompt(q, kvc, kvs, kvw, gates, pe_cmp[l], w_cmp1[l], w_cmp2[l], slopes)
        o_gm, _ = _gmlp(u, v, gm_ln_g[l], gm_ln_b[l], w_spatial[l], b_spatial[l])
        xp = xp + _mix_out(o_nsa, o_gm, out_norm_g[l], w_out[l])
        xp = xp + _hier_moe(_rms(xp, norm2_g[l]), w_router_group[l], b_router_group[l], w_router_expert[l],
                            b_router_expert[l], w_gate_e[l], w_up_e[l], w_down_e[l])
        T = kvw.shape[1]
        cmp_p.append(kvc)
        sel_p.append(kvs)
        win_p.append(kvw[:, T - min(WINDOW, T):])
        q, kvc, kvs, kvw, gates, u, v = _split_proj(_rms(xs, norm1_g[l]) @ w_in[l])
        o_nsa = _nsa_sample(q, kvc, kvs, kvw, gates, cache_cmp_kv[l], cache_sel_kv[l], state_win_kv[l],
                            page_table, pe_cmp[l], w_cmp1[l], w_cmp2[l], slopes)
        o_gm, vn = _gmlp(u, v, gm_ln_g[l], gm_ln_b[l], w_spatial[l], b_spatial[l])
        xs = xs + _mix_out(o_nsa, o_gm, out_norm_g[l], w_out[l])
        xs = xs + _hier_moe(_rms(xs, norm2_g[l]), w_router_group[l], b_router_group[l], w_router_expert[l],
                            b_router_expert[l], w_gate_e[l], w_up_e[l], w_down_e[l])
        cmp_s.append(kvc)
        sel_s.append(kvs)
        win_s.append(kvw)
        gm_s.append(vn)
    y_prompt = _rms(xp, final_norm_g)
    y_sample = _rms(xs, final_norm_g)
    new_cmp_kv_prompt = jnp.stack(cmp_p)
    new_cmp_kv_sample = jnp.stack(cmp_s)
    new_sel_kv_prompt = jnp.stack(sel_p)
    new_sel_kv_sample = jnp.stack(sel_s)
    new_win_kv_prompt = jnp.stack(win_p)
    new_win_kv_sample = jnp.stack(win_s)
    new_gm_v_sample = jnp.stack(gm_s)
    return (y_prompt, y_sample, new_cmp_kv_prompt, new_cmp_kv_sample, new_sel_kv_prompt, new_sel_kv_sample,
            new_win_kv_prompt, new_win_kv_sample, new_gm_v_sample)
```

```python
import functools

import jax
import jax.numpy as jnp
import numpy as np
from jax import lax
from jax.experimental import pallas as pl
from jax.experimental.pallas import tpu as pltpu

F32 = jnp.float32
BF16 = jnp.bfloat16

D_MODEL = 1024
BATCH = 8
SEQ = 2048
DEPTH = 4
DEC_BATCH = 32
DEC_SEQ = 4
PAST_LEN = 8192
PAGE_SIZE = 128
NSA_WIDTH = 512
GM_WIDTH = 512
HEAD_DIM = 64
N_HEADS = 8
N_KV = 2
HPG = 4
KV_WIDTH = 128
L_BLK = 64
N_SEL = 16
WINDOW = 512
CMP_HIDDEN = 128
GM_DIM = 64
GM_GROUPS = 8
CHUNK = 128
N_EGROUPS = 4
EXPERTS_PER_GROUP = 4
N_EXPERTS = 16
EXPERT_HIDDEN = 512
EPS = 1e-6
NEG_INF = -1e30

LANES = 128
SUBLANES = 8
VMEM_LIMIT_BYTES = 56 * 1024 * 1024

N_PROMPT = BATCH * SEQ
N_SAMPLE = DEC_BATCH * DEC_SEQ
ROW_TILE = 512
N_TOK = N_PROMPT + ROW_TILE
N_ROW_TILES = N_TOK // ROW_TILE
SAMPLE_TILE = N_PROMPT // ROW_TILE

QKV_COLS = NSA_WIDTH + 6 * KV_WIDTH
GATE_COLS = 3 * N_HEADS
U_OFF = QKV_COLS + LANES
V_OFF = U_OFF + GM_WIDTH
PROJ_COLS = V_OFF + GM_WIDTH
N_KV_SLABS = 12

PACK_COLS = D_MODEL + LANES
CLS_LANE = 4
RANK_LANE = 5
MOE_TILE = 256
N_MOE_TILES = N_TOK // MOE_TILE
N_MOE_ITEMS = N_MOE_TILES + N_EGROUPS

Q_TILE = 128
SEL_K_TILE = 512
WIN_SPAN = WINDOW + Q_TILE
N_BLK_PROMPT = SEQ // L_BLK
N_BLK_PAST = PAST_LEN // L_BLK
N_PAGES = PAST_LEN // PAGE_SIZE
BLK_PER_PAGE = PAGE_SIZE // L_BLK
L_SUB = 8


def _slope(head):
    return float(2.0 ** (-8.0 * (head + 1) / N_HEADS))


def _params(semantics=None):
    return pltpu.CompilerParams(dimension_semantics=semantics, vmem_limit_bytes=VMEM_LIMIT_BYTES)


def _rms(x, g):
    return x * lax.rsqrt(jnp.mean(x * x, axis=-1, keepdims=True) + EPS) * g


def _split_bf16(x):
    hi = x.astype(BF16)
    lo = (x - hi.astype(F32)).astype(BF16)
    return hi, lo


def _dot(a, b):
    return jnp.dot(a, b, preferred_element_type=F32)


def _dot_nt(a, b):
    return lax.dot_general(a, b, (((1,), (1,)), ((), ())), preferred_element_type=F32)


def _dot_tn(a, b):
    return lax.dot_general(a, b, (((0,), (0,)), ((), ())), preferred_element_type=F32)


def _inproj_kernel(x_ref, g1_ref, w_ref, lng_ref, lnb_ref, seg_ref, wmix_ref, bmix_ref,
                   q_ref, kvb_ref, kvc_ref, kvs_ref, kvw_ref, gates_ref, ogm_ref, vn_ref):
    x = x_ref[...]
    xn = _rms(x, g1_ref[...])
    p = _dot(xn.astype(BF16), w_ref[...])
    scale = HEAD_DIM ** -0.5
    for h in range(N_HEADS):
        q_ref[h] = (p[:, h * HEAD_DIM:(h + 1) * HEAD_DIM] * scale).astype(BF16)
    kv = p[:, NSA_WIDTH:QKV_COLS]
    kvc_ref[...] = kv[:, 0:2 * KV_WIDTH]
    kvs_ref[...] = kv[:, 2 * KV_WIDTH:4 * KV_WIDTH]
    kvw_ref[...] = kv[:, 4 * KV_WIDTH:6 * KV_WIDTH]
    for j in range(N_KV_SLABS):
        kvb_ref[j] = kv[:, j * HEAD_DIM:(j + 1) * HEAD_DIM].astype(BF16)
    gates_ref[...] = jax.nn.sigmoid(p[:, QKV_COLS:QKV_COLS + LANES])

    u = jax.nn.gelu(p[:, U_OFF:U_OFF + GM_WIDTH])
    vf = jax.nn.gelu(p[:, V_OFF:V_OFF + GM_WIDTH])
    seg = seg_ref[...]

    def seg_mean(a):
        hi, lo = _split_bf16(a)
        cols = []
        for s in range(GM_WIDTH // LANES):
            sl = slice(s * LANES, (s + 1) * LANES)
            cols.append(_dot(hi[:, sl], seg) + _dot(lo[:, sl], seg))
        return jnp.concatenate(cols, axis=1) * (1.0 / GM_DIM)

    d = vf - seg_mean(vf)
    var = seg_mean(d * d)
    vn = d * lax.rsqrt(var + EPS) * lng_ref[...] + lnb_ref[...]
    vn_ref[...] = vn
    vnb = vn.astype(BF16)

    row = lax.broadcasted_iota(jnp.int32, (CHUNK, CHUNK), 0)
    col = lax.broadcasted_iota(jnp.int32, (CHUNK, CHUNK), 1)
    causal = row >= col
    left = lax.broadcasted_iota(jnp.int32, (CHUNK, LANES), 1) < GM_DIM
    wm = [jnp.where(causal, wmix_ref[0, g], jnp.zeros((), BF16)) for g in range(GM_GROUPS)]
    tm = x.shape[0]
    for c in range(tm // CHUNK):
        rs = slice(c * CHUNK, (c + 1) * CHUNK)
        for s in range(GM_WIDTH // LANES):
            sl = slice(s * LANES, (s + 1) * LANES)
            vslab = vnb[rs, sl]
            mixed = jnp.where(left, _dot(wm[2 * s], vslab), _dot(wm[2 * s + 1], vslab))
            ogm_ref[rs, sl] = u[rs, sl] * (mixed + bmix_ref[0, :, sl])


def _inproj(x_all, g1, w, lng, lnb, seg, wmix, bmix):
    tm = ROW_TILE
    row = lambda i: (i, 0)
    fixed2 = lambda i: (0, 0)
    mix_idx = lambda i: (jnp.where(i == SAMPLE_TILE, 1, 0), 0, 0, 0)
    bmix_idx = lambda i: (jnp.where(i == SAMPLE_TILE, 1, 0), 0, 0)
    out_shape = (
        jax.ShapeDtypeStruct((N_HEADS, N_TOK, HEAD_DIM), BF16),
        jax.ShapeDtypeStruct((N_KV_SLABS, N_TOK, HEAD_DIM), BF16),
        jax.ShapeDtypeStruct((N_TOK, 2 * KV_WIDTH), F32),
        jax.ShapeDtypeStruct((N_TOK, 2 * KV_WIDTH), F32),
        jax.ShapeDtypeStruct((N_TOK, 2 * KV_WIDTH), F32),
        jax.ShapeDtypeStruct((N_TOK, LANES), F32),
        jax.ShapeDtypeStruct((N_TOK, GM_WIDTH), F32),
        jax.ShapeDtypeStruct((tm, GM_WIDTH), F32),
    )
    return pl.pallas_call(
        _inproj_kernel,
        grid=(N_ROW_TILES,),
        in_specs=[
            pl.BlockSpec((tm, D_MODEL), row),
            pl.BlockSpec((1, D_MODEL), fixed2),
            pl.BlockSpec((D_MODEL, PROJ_COLS), fixed2),
            pl.BlockSpec((1, GM_WIDTH), fixed2),
            pl.BlockSpec((1, GM_WIDTH), fixed2),
            pl.BlockSpec((LANES, LANES), fixed2),
            pl.BlockSpec((1, GM_GROUPS, CHUNK, CHUNK), mix_idx),
            pl.BlockSpec((1, CHUNK, GM_WIDTH), bmix_idx),
        ],
        out_specs=(
            pl.BlockSpec((N_HEADS, tm, HEAD_DIM), lambda i: (0, i, 0)),
            pl.BlockSpec((N_KV_SLABS, tm, HEAD_DIM), lambda i: (0, i, 0)),
            pl.BlockSpec((tm, 2 * KV_WIDTH), row),
            pl.BlockSpec((tm, 2 * KV_WIDTH), row),
            pl.BlockSpec((tm, 2 * KV_WIDTH), row),
            pl.BlockSpec((tm, LANES), row),
            pl.BlockSpec((tm, GM_WIDTH), row),
            pl.BlockSpec((tm, GM_WIDTH), fixed2),
        ),
        out_shape=out_shape,
        compiler_params=_params(("arbitrary",)),
        name="inproj_gmlp",
    )(x_all, g1, w, lng, lnb, seg, wmix, bmix)


def _mixout_kernel(x_ref, onsa_p_ref, onsa_s_ref, ogm_ref, go_ref, wo_ref, g2_ref, wrh_ref, wrl_ref,
                   br_ref, tril_ref, xnew_ref, packed_ref, meta_ref, counts_ref, carry_ref):
    i = pl.program_id(0)

    @pl.when(i == 0)
    def _():
        carry_ref[...] = jnp.zeros_like(carry_ref)

    go = go_ref[...]
    onsa = jnp.where(i == SAMPLE_TILE, onsa_s_ref[...], onsa_p_ref[...])
    an = _rms(onsa, go[:, :NSA_WIDTH])
    bn = _rms(ogm_ref[...], go[:, NSA_WIDTH:])
    y = _dot(an.astype(BF16), wo_ref[:NSA_WIDTH, :]) + _dot(bn.astype(BF16), wo_ref[NSA_WIDTH:, :])
    xnew = x_ref[...] + y
    xnew_ref[...] = xnew
    xn2 = _rms(xnew, g2_ref[...])
    packed_ref[:, :D_MODEL] = xn2

    hi, lo = _split_bf16(xn2)
    wrh = wrh_ref[...]
    lg = _dot(hi, wrh) + _dot(lo, wrh) + _dot(hi, wrl_ref[...]) + br_ref[...]
    tm = lg.shape[0]
    lane = lax.broadcasted_iota(jnp.int32, (tm, LANES), 1)
    neg = jnp.float32(-jnp.inf)

    def first_argmax(v, vmax):
        return jnp.min(jnp.where(v == vmax, lane, LANES), axis=-1, keepdims=True)

    lgm = jnp.where(lane < N_EGROUPS, lg, neg)
    gmax = jnp.max(lgm, axis=-1, keepdims=True)
    grp = first_argmax(lgm, gmax)
    g_gate = 1.0 / jnp.sum(jnp.exp(lgm - gmax), axis=-1, keepdims=True)
    le = jnp.zeros_like(lg)
    for g in range(N_EGROUPS):
        shift = LANES - (N_EGROUPS + g * EXPERTS_PER_GROUP)
        le = jnp.where(grp == g, pltpu.roll(lg, shift, 1), le)
    in_grp = lane < EXPERTS_PER_GROUP
    lem = jnp.where(in_grp, le, neg)
    ee = jnp.exp(lem - jnp.max(lem, axis=-1, keepdims=True))
    pe = ee / jnp.sum(ee, axis=-1, keepdims=True)
    pm = jnp.where(in_grp, pe, -1.0)
    p1 = jnp.max(pm, axis=-1, keepdims=True)
    i1 = first_argmax(pm, p1)
    pm2 = jnp.where(lane == i1, -1.0, pm)
    p2 = jnp.max(pm2, axis=-1, keepdims=True)
    i2 = first_argmax(pm2, p2)
    psum = p1 + p2
    cw = jnp.where(lane == i1, g_gate * p1 / psum, 0.0) + jnp.where(lane == i2, g_gate * p2 / psum, 0.0)

    cls = grp
    onehot = jnp.where(lane == cls, 1.0, 0.0)
    before = _dot(tril_ref[...], onehot.astype(BF16)) + carry_ref[...]
    rank = jnp.sum(jnp.where(lane == cls, before, 0.0), axis=-1, keepdims=True)
    carry_ref[...] = carry_ref[...] + jnp.sum(onehot, axis=0, keepdims=True)
    counts_ref[...] = carry_ref[...]

    meta = cw + jnp.where(lane == CLS_LANE, cls.astype(F32), 0.0) + jnp.where(lane == RANK_LANE, rank, 0.0)
    packed_ref[:, D_MODEL:] = meta
    meta_ref[0] = jnp.transpose(meta)[:SUBLANES, :]


def _mixout(x_all, o_nsa_p, o_nsa_s, o_gm, go, wo, g2, wrh, wrl, br, tril):
    tm = ROW_TILE
    row = lambda i: (i, 0)
    fixed2 = lambda i: (0, 0)
    return pl.pallas_call(
        _mixout_kernel,
        grid=(N_ROW_TILES,),
        in_specs=[
            pl.BlockSpec((tm, D_MODEL), row),
            pl.BlockSpec((tm, NSA_WIDTH), lambda i: (jnp.minimum(i, SAMPLE_TILE - 1), 0)),
            pl.BlockSpec((tm, NSA_WIDTH), fixed2),
            pl.BlockSpec((tm, GM_WIDTH), row),
            pl.BlockSpec((1, D_MODEL), fixed2),
            pl.BlockSpec((D_MODEL, D_MODEL), fixed2),
            pl.BlockSpec((1, D_MODEL), fixed2),
            pl.BlockSpec((D_MODEL, LANES), fixed2),
            pl.BlockSpec((D_MODEL, LANES), fixed2),
            pl.BlockSpec((1, LANES), fixed2),
            pl.BlockSpec((tm, tm), fixed2),
        ],
        out_specs=(
            pl.BlockSpec((tm, D_MODEL), row),
            pl.BlockSpec((tm, PACK_COLS), row),
            pl.BlockSpec((1, SUBLANES, tm), lambda i: (i, 0, 0)),
            pl.BlockSpec((1, LANES), fixed2),
        ),
        out_shape=(
            jax.ShapeDtypeStruct((N_TOK, D_MODEL), F32),
            jax.ShapeDtypeStruct((N_TOK, PACK_COLS), F32),
            jax.ShapeDtypeStruct((N_ROW_TILES, SUBLANES, tm), F32),
            jax.ShapeDtypeStruct((1, LANES), F32),
        ),
        scratch_shapes=[pltpu.VMEM((1, LANES), F32)],
        compiler_params=_params(("arbitrary",)),
        name="mixout_router",
    )(x_all, o_nsa_p, o_nsa_s, o_gm, go, wo, g2, wrh, wrl, br, tril)


def _row_copy(src_ref, src_row, dst_ref, dst_row, sem):
    return pltpu.make_async_copy(src_ref.at[pl.ds(src_row, 1)], dst_ref.at[pl.ds(dst_row, 1)], sem)


def _scatter_kernel(pos_ref, packed_ref, out_ref, sem):
    base = pl.program_id(0) * ROW_TILE

    def start(r, c):
        _row_copy(packed_ref, r, out_ref, pos_ref[base + r], sem).start()
        return c

    lax.fori_loop(0, ROW_TILE, start, 0)

    def wait(r, c):
        _row_copy(packed_ref, r, out_ref, pos_ref[base + r], sem).wait()
        return c

    lax.fori_loop(0, ROW_TILE, wait, 0)


def _scatter_rows(pos, packed):
    return pl.pallas_call(
        _scatter_kernel,
        grid_spec=pltpu.PrefetchScalarGridSpec(
            num_scalar_prefetch=1,
            grid=(N_ROW_TILES,),
            in_specs=[pl.BlockSpec((ROW_TILE, PACK_COLS), lambda i, pos: (i, 0))],
            out_specs=pl.BlockSpec(memory_space=pl.ANY),
            scratch_shapes=[pltpu.SemaphoreType.DMA(())],
        ),
        out_shape=jax.ShapeDtypeStruct((N_TOK, PACK_COLS), F32),
        compiler_params=_params(("arbitrary",)),
        name="moe_scatter",
    )(pos, packed)


def _expert_kernel(tile_ref, grp_ref, lo_ref, hi_ref, first_ref, xs_ref, wg_ref, wu_ref, wd_ref, y_ref):
    j = pl.program_id(0)
    x = xs_ref[:, :D_MODEL].astype(BF16)
    cw = xs_ref[:, D_MODEL:]
    acc = jnp.zeros((MOE_TILE, D_MODEL), F32)
    for e in range(EXPERTS_PER_GROUP):
        h = jax.nn.silu(_dot(x, wg_ref[e])) * _dot(x, wu_ref[e])
        acc = acc + cw[:, e:e + 1] * _dot(h.astype(BF16), wd_ref[e])
    row = lax.broadcasted_iota(jnp.int32, (MOE_TILE, 1), 0)
    mine = (row >= lo_ref[j]) & (row < hi_ref[j])

    @pl.when(first_ref[j] == 1)
    def _():
        y_ref[...] = jnp.where(mine, acc, 0.0)

    @pl.when(first_ref[j] == 0)
    def _():
        y_ref[...] = jnp.where(mine, acc, y_ref[...])


def _expert_mlp(items, xs, wg, wu, wd):
    tile, grp, lo, hi, first = items
    wmap = lambda j, tile, grp, lo, hi, first: (grp[j], 0, 0)
    rmap = lambda j, tile, grp, lo, hi, first: (tile[j], 0)
    return pl.pallas_call(
        _expert_kernel,
        grid_spec=pltpu.PrefetchScalarGridSpec(
            num_scalar_prefetch=5,
            grid=(N_MOE_ITEMS,),
            in_specs=[
                pl.BlockSpec((MOE_TILE, PACK_COLS), rmap),
                pl.BlockSpec((EXPERTS_PER_GROUP, D_MODEL, EXPERT_HIDDEN), wmap),
                pl.BlockSpec((EXPERTS_PER_GROUP, D_MODEL, EXPERT_HIDDEN), wmap),
                pl.BlockSpec((EXPERTS_PER_GROUP, EXPERT_HIDDEN, D_MODEL), wmap),
            ],
            out_specs=pl.BlockSpec((MOE_TILE, D_MODEL), rmap),
        ),
        out_shape=jax.ShapeDtypeStruct((N_TOK, D_MODEL), F32),
        compiler_params=_params(("arbitrary",)),
        name="moe_experts",
    )(tile, grp, lo, hi, first, xs, wg, wu, wd)


def _unsort_kernel(final, pos_ref, x_ref, ys_ref, *rest):
    if final:
        gf_ref, xout_ref, yfin_ref, buf_ref, sem = rest
    else:
        xout_ref, buf_ref, sem = rest
    base = pl.program_id(0) * ROW_TILE

    def start(r, c):
        _row_copy(ys_ref, pos_ref[base + r], buf_ref, r, sem).start()
        return c

    lax.fori_loop(0, ROW_TILE, start, 0)

    def wait(r, c):
        _row_copy(ys_ref, pos_ref[base + r], buf_ref, r, sem).wait()
        return c

    lax.fori_loop(0, ROW_TILE, wait, 0)
    xo = x_ref[...] + buf_ref[...]
    xout_ref[...] = xo
    if final:
        yfin_ref[...] = _rms(xo, gf_ref[...])


def _unsort_residual(pos, x_all, ys, final_g=None):
    final = final_g is not None
    row = lambda i, pos: (i, 0)
    in_specs = [pl.BlockSpec((ROW_TILE, D_MODEL), row), pl.BlockSpec(memory_space=pl.ANY)]
    out_specs = [pl.BlockSpec((ROW_TILE, D_MODEL), row)]
    out_shape = [jax.ShapeDtypeStruct((N_TOK, D_MODEL), F32)]
    args = [pos, x_all, ys]
    if final:
        in_specs.append(pl.BlockSpec((1, D_MODEL), lambda i, pos: (0, 0)))
        out_specs.append(pl.BlockSpec((ROW_TILE, D_MODEL), row))
        out_shape.append(jax.ShapeDtypeStruct((N_TOK, D_MODEL), F32))
        args.append(final_g)
    return pl.pallas_call(
        functools.partial(_unsort_kernel, final),
        grid_spec=pltpu.PrefetchScalarGridSpec(
            num_scalar_prefetch=1,
            grid=(N_ROW_TILES,),
            in_specs=in_specs,
            out_specs=tuple(out_specs),
            scratch_shapes=[pltpu.VMEM((ROW_TILE, D_MODEL), F32), pltpu.SemaphoreType.DMA(())],
        ),
        out_shape=tuple(out_shape),
        compiler_params=_params(("arbitrary",)),
        name="moe_unsort_final" if final else "moe_unsort",
    )(*args)


def _moe_plan(meta, counts):
    cls = meta[:, CLS_LANE, :].reshape(N_TOK).astype(jnp.int32)
    rank = meta[:, RANK_LANE, :].reshape(N_TOK).astype(jnp.int32)
    cnt = counts[0, :N_EGROUPS].astype(jnp.int32)
    end = jnp.cumsum(cnt)
    off = end - cnt
    pos = off[cls] + rank

    tile_starts = jnp.arange(N_MOE_TILES, dtype=jnp.int32) * MOE_TILE
    cand = jnp.concatenate([tile_starts, off])
    ok = jnp.concatenate([jnp.ones((N_MOE_TILES,), bool), (cnt > 0) & (off % MOE_TILE != 0)])
    cand = jnp.where(ok, cand, N_TOK)
    order = jnp.sum((cand[None, :] < cand[:, None]) & ok[None, :], axis=1)
    slot = jnp.arange(N_MOE_ITEMS, dtype=jnp.int32)
    hit = (order[None, :] == slot[:, None]) & ok[None, :]
    start = jnp.where(jnp.any(hit, axis=1), jnp.sum(jnp.where(hit, cand[None, :], 0), axis=1), N_TOK)
    live = start < N_TOK
    row0 = jnp.minimum(start, N_TOK - 1)
    tile = row0 // MOE_TILE
    grp = jnp.sum(end[None, :] <= row0[:, None], axis=1).astype(jnp.int32)
    grp = jnp.minimum(grp, N_EGROUPS - 1)
    lo = jnp.where(live, start - tile * MOE_TILE, MOE_TILE)
    hi = jnp.where(live, jnp.minimum(end[grp], (tile + 1) * MOE_TILE) - tile * MOE_TILE, MOE_TILE)
    prev_tile = jnp.concatenate([jnp.full((1,), -1, jnp.int32), tile[:-1]])
    first = (live & (tile != prev_tile)).astype(jnp.int32)
    items = tuple(a.astype(jnp.int32) for a in (tile, grp, lo, hi, first))
    return pos.astype(jnp.int32), items


def _compress_accumulate(load_cols, pe_ref, w1_ref, acc_ref, li):
    for c in range(2):
        w = w1_ref[c, pl.ds(li * HEAD_DIM, HEAD_DIM), :]
        pe = pe_ref[c, pl.ds(li, 1), :]
        for g in range(N_KV):
            cs = (c * N_KV + g) * HEAD_DIM
            xl = load_cols(slice(cs, cs + HEAD_DIM)) + pe
            acc_ref[c * N_KV + g] += _dot(xl.astype(BF16), w)


def _compress_finish(acc_ref, w2_ref, out_ref):
    for c in range(2):
        for g in range(N_KV):
            h = jax.nn.gelu(acc_ref[c * N_KV + g])
            out_ref[c * N_KV + g] = _dot(h.astype(BF16), w2_ref[c])


def _compress_kernel(blocks_ref, pe_ref, w1_ref, w2_ref, out_ref, acc_ref):
    step = pl.program_id(0)

    @pl.when(step == 0)
    def _():
        acc_ref[...] = jnp.zeros_like(acc_ref)

    for li in range(L_SUB):
        _compress_accumulate(lambda cs, li=li: blocks_ref[:, li, cs], pe_ref, w1_ref, acc_ref, li)

    @pl.when(step == pl.num_programs(0) - 1)
    def _():
        _compress_finish(acc_ref, w2_ref, out_ref)


def _compress_prompt(kvc, pe, w1, w2):
    nb = BATCH * N_BLK_PROMPT
    blocks = kvc.reshape(N_TOK // L_BLK, L_BLK, 2 * KV_WIDTH)
    return pl.pallas_call(
        _compress_kernel,
        grid=(L_BLK // L_SUB,),
        in_specs=[
            pl.BlockSpec((nb, L_SUB, 2 * KV_WIDTH), lambda s: (0, s, 0)),
            pl.BlockSpec((2, L_SUB, HEAD_DIM), lambda s: (0, s, 0)),
            pl.BlockSpec((2, L_SUB * HEAD_DIM, CMP_HIDDEN), lambda s: (0, s, 0)),
            pl.BlockSpec((2, CMP_HIDDEN, HEAD_DIM), lambda s: (0, 0, 0)),
        ],
        out_specs=pl.BlockSpec((2 * N_KV, nb, HEAD_DIM), lambda s: (0, 0, 0)),
        out_shape=jax.ShapeDtypeStruct((2 * N_KV, nb, HEAD_DIM), F32),
        scratch_shapes=[pltpu.VMEM((2 * N_KV, nb, CMP_HIDDEN), F32)],
        compiler_params=_params(("arbitrary",)),
        name="compress_prompt",
    )(blocks, pe, w1, w2)


def _cmp_select_kernel(q_ref, comp_ref, gates_ref, ocmp_ref, sel_ref):
    nb = N_BLK_PROMPT
    tpos = lax.broadcasted_iota(jnp.int32, (nb, SEQ), 1)
    nidx = lax.broadcasted_iota(jnp.int32, (nb, SEQ), 0)
    dist = tpos - (nidx * L_BLK + (L_BLK - 1))
    seen = dist >= 0
    distf = dist.astype(F32)
    cand = nidx < tpos // L_BLK
    gates = gates_ref[...]
    for g in range(N_KV):
        kc = comp_ref[g].astype(BF16)
        vc = comp_ref[N_KV + g].astype(BF16)
        imp = jnp.zeros((nb, SEQ), F32)
        for hh in range(HPG):
            head = g * HPG + hh
            s = _dot_nt(kc, q_ref[head]) - _slope(head) * distf
            s = jnp.where(seen, s, NEG_INF)
            e = jnp.exp(s - jnp.max(s, axis=0, keepdims=True))
            p = e / jnp.sum(e, axis=0, keepdims=True)
            p = jnp.where(seen, p, 0.0)
            imp = imp + p
            o = _dot_tn(p.astype(BF16), vc)
            ocmp_ref[:, head * HEAD_DIM:(head + 1) * HEAD_DIM] = o * gates[:, head:head + 1]
        score = jnp.where(cand, imp, -1.0)
        rank = jnp.zeros((nb, SEQ), jnp.int32)
        for m in range(nb):
            rowm = score[m:m + 1, :]
            ahead = (rowm > score) | ((rowm == score) & (nidx > m))
            rank = rank + jnp.where(ahead, 1, 0)
        chosen = cand & (rank < N_SEL - 1) & (score >= 0.0)
        sel_ref[0, g] = jnp.where(chosen, 1.0, 0.0).astype(BF16)


def _cmp_select_prompt(q_hm, comp, gates):
    return pl.pallas_call(
        _cmp_select_kernel,
        grid=(BATCH,),
        in_specs=[
            pl.BlockSpec((N_HEADS, SEQ, HEAD_DIM), lambda b: (0, b, 0)),
            pl.BlockSpec((2 * N_KV, N_BLK_PROMPT, HEAD_DIM), lambda b: (0, b, 0)),
            pl.BlockSpec((SEQ, LANES), lambda b: (b, 0)),
        ],
        out_specs=(
            pl.BlockSpec((SEQ, NSA_WIDTH), lambda b: (b, 0)),
            pl.BlockSpec((1, N_KV, N_BLK_PROMPT, SEQ), lambda b: (b, 0, 0, 0)),
        ),
        out_shape=(
            jax.ShapeDtypeStruct((N_PROMPT, NSA_WIDTH), F32),
            jax.ShapeDtypeStruct((BATCH, N_KV, N_BLK_PROMPT, SEQ), BF16),
        ),
        compiler_params=_params(("arbitrary",)),
        name="cmp_select_prompt",
    )(q_hm, comp, gates)


def _softmax_rows(s3, allowed):
    s3 = jnp.where(allowed[None], s3, NEG_INF)
    m = jnp.max(s3, axis=-1, keepdims=True)
    e = jnp.exp(s3 - m)
    return e, jnp.sum(e, axis=-1, keepdims=True)


def _sel_win_kernel(q_ref, ksel_ref, kwin_ref, sel_ref, gates_ref, ocmp_ref, out_ref):
    i = pl.program_id(1)
    tq = Q_TILE
    tk = SEL_K_TILE
    q0 = i * tq
    qpos = q0 + lax.broadcasted_iota(jnp.int32, (tq, 1), 0)
    gates = gates_ref[...]
    ocmp = ocmp_ref[...]
    n_chunks = q0 // tk + 1
    win_start = pl.multiple_of(jnp.maximum(q0 - WINDOW, 0), Q_TILE)

    for g in range(N_KV):
        qg = q_ref[g * HPG:(g + 1) * HPG].reshape(HPG * tq, HEAD_DIM)
        slopes = jnp.concatenate(
            [jnp.full((1, 1, 1), _slope(g * HPG + hh), F32) for hh in range(HPG)], axis=0)
        selg = sel_ref[0, g]

        def chunk(j, carry):
            m_i, l_i, acc = carry
            k0 = pl.multiple_of(j * tk, tk)
            k = ksel_ref[g, pl.ds(k0, tk), :]
            v = ksel_ref[N_KV + g, pl.ds(k0, tk), :]
            kpos = k0 + lax.broadcasted_iota(jnp.int32, (1, tk), 1)
            kblk = kpos // L_BLK
            expand = jnp.where(lax.broadcasted_iota(jnp.int32, (N_BLK_PROMPT, tk), 0) == kblk, 1.0, 0.0)
            picked = _dot_tn(selg, expand.astype(BF16))
            allowed = (picked > 0.5) | ((kblk == qpos // L_BLK) & (kpos <= qpos))
            s3 = _dot_nt(qg, k).reshape(HPG, tq, tk) + slopes * kpos.astype(F32)[None]
            s3 = jnp.where(allowed[None], s3, NEG_INF)
            m_new = jnp.maximum(m_i, jnp.max(s3, axis=-1, keepdims=True))
            alpha = jnp.exp(m_i - m_new)
            e = jnp.exp(s3 - m_new)
            l_new = alpha * l_i + jnp.sum(e, axis=-1, keepdims=True)
            pv = _dot(e.reshape(HPG * tq, tk).astype(BF16), v).reshape(HPG, tq, HEAD_DIM)
            return m_new, l_new, alpha * acc + pv

        init = (jnp.full((HPG, tq, 1), NEG_INF, F32), jnp.zeros((HPG, tq, 1), F32),
                jnp.zeros((HPG, tq, HEAD_DIM), F32))
        _, l_sel, acc_sel = lax.fori_loop(0, n_chunks, chunk, init)
        o_sel = acc_sel / l_sel

        kw = kwin_ref[g, pl.ds(win_start, WIN_SPAN), :]
        vw = kwin_ref[N_KV + g, pl.ds(win_start, WIN_SPAN), :]
        kpos = win_start + lax.broadcasted_iota(jnp.int32, (1, WIN_SPAN), 1)
        d = qpos - kpos
        allowed = (d >= 0) & (d <= WINDOW)
        s3 = _dot_nt(qg, kw).reshape(HPG, tq, WIN_SPAN) + slopes * kpos.astype(F32)[None]
        e, l_win = _softmax_rows(s3, allowed)
        o_win = _dot(e.reshape(HPG * tq, WIN_SPAN).astype(BF16), vw).reshape(HPG, tq, HEAD_DIM) / l_win

        for hh in range(HPG):
            head = g * HPG + hh
            cs = slice(head * HEAD_DIM, (head + 1) * HEAD_DIM)
            g_sel = gates[:, N_HEADS + head:N_HEADS + head + 1]
            g_win = gates[:, 2 * N_HEADS + head:2 * N_HEADS + head + 1]
            out_ref[:, cs] = ocmp[:, cs] + g_sel * o_sel[hh] + g_win * o_win[hh]


def _sel_win_prompt(q_hm, kvb, sel, gates, ocmp):
    nq = SEQ // Q_TILE
    rows = lambda b, i: (b * nq + i, 0)
    return pl.pallas_call(
        _sel_win_kernel,
        grid=(BATCH, nq),
        in_specs=[
            pl.BlockSpec((N_HEADS, Q_TILE, HEAD_DIM), lambda b, i: (0, b * nq + i, 0)),
            pl.BlockSpec((2 * N_KV, SEQ, HEAD_DIM), lambda b, i: (1, b, 0)),
            pl.BlockSpec((2 * N_KV, SEQ, HEAD_DIM), lambda b, i: (2, b, 0)),
            pl.BlockSpec((1, N_KV, N_BLK_PROMPT, Q_TILE), lambda b, i: (b, 0, 0, i)),
            pl.BlockSpec((Q_TILE, LANES), rows),
            pl.BlockSpec((Q_TILE, NSA_WIDTH), rows),
        ],
        out_specs=pl.BlockSpec((Q_TILE, NSA_WIDTH), rows),
        out_shape=jax.ShapeDtypeStruct((N_PROMPT, NSA_WIDTH), F32),
        compiler_params=_params(("arbitrary", "arbitrary")),
        name="sel_win_prompt",
    )(q_hm, kvb, kvb, sel, gates, ocmp)


N_POOL = DEC_BATCH * N_PAGES + max(1, DEC_BATCH * N_PAGES // 4)
SEL_SLOTS = N_SEL
IDX_LANES = 16
WIN_ROWS = WINDOW + 2 * DEC_SEQ


def _compress_sample_kernel(pt_ref, cache_ref, pe_ref, w1_ref, w2_ref, out_ref, buf_ref, acc_ref, sem):
    layer = pl.program_id(0)
    b = pl.program_id(1)
    step = layer * DEC_BATCH + b
    slot = step % 2

    def page_copy(lv, bv, p, sl):
        page = lv * N_POOL + pt_ref[bv * N_PAGES + p]
        return pltpu.make_async_copy(cache_ref.at[page], buf_ref.at[sl, pl.ds(p * BLK_PER_PAGE, BLK_PER_PAGE)],
                                     sem.at[sl])

    def start_all(lv, bv, sl):
        def body(p, c):
            page_copy(lv, bv, p, sl).start()
            return c
        lax.fori_loop(0, N_PAGES, body, 0)

    @pl.when(step == 0)
    def _():
        start_all(layer, b, slot)

    @pl.when(step + 1 < DEPTH * DEC_BATCH)
    def _():
        nxt = step + 1
        start_all(nxt // DEC_BATCH, nxt % DEC_BATCH, 1 - slot)

    def wait_body(p, c):
        page_copy(layer, b, p, slot).wait()
        return c

    lax.fori_loop(0, N_PAGES, wait_body, 0)

    acc_ref[...] = jnp.zeros_like(acc_ref)
    blocks_ref = buf_ref.at[slot]
    pe_l = pe_ref.at[0]
    w1_l = w1_ref.at[0]

    def body(l8, c):
        for k in range(L_SUB):
            _compress_accumulate(lambda cs, k=k: blocks_ref[:, l8, k, cs], pe_l, w1_l, acc_ref, l8 * L_SUB + k)
        return c

    lax.fori_loop(0, L_BLK // L_SUB, body, 0)
    _compress_finish(acc_ref, w2_ref.at[0], out_ref.at[0, 0])


def _compress_sample(page_table, cache_cmp, pe, w1, w2):
    cache = cache_cmp.reshape(DEPTH * N_POOL, BLK_PER_PAGE, L_BLK // L_SUB, L_SUB, 2 * KV_WIDTH)
    pt = page_table.reshape(DEC_BATCH * N_PAGES)
    return pl.pallas_call(
        _compress_sample_kernel,
        grid_spec=pltpu.PrefetchScalarGridSpec(
            num_scalar_prefetch=1,
            grid=(DEPTH, DEC_BATCH),
            in_specs=[
                pl.BlockSpec(memory_space=pl.ANY),
                pl.BlockSpec((1, 2, L_BLK, HEAD_DIM), lambda l, b, pt: (l, 0, 0, 0)),
                pl.BlockSpec((1, 2, L_BLK * HEAD_DIM, CMP_HIDDEN), lambda l, b, pt: (l, 0, 0, 0)),
                pl.BlockSpec((1, 2, CMP_HIDDEN, HEAD_DIM), lambda l, b, pt: (l, 0, 0, 0)),
            ],
            out_specs=pl.BlockSpec((1, 1, 2 * N_KV, N_BLK_PAST, HEAD_DIM), lambda l, b, pt: (l, b, 0, 0, 0)),
            scratch_shapes=[
                pltpu.VMEM((2, N_BLK_PAST, L_BLK // L_SUB, L_SUB, 2 * KV_WIDTH), F32),
                pltpu.VMEM((2 * N_KV, N_BLK_PAST, CMP_HIDDEN), F32),
                pltpu.SemaphoreType.DMA((2,)),
            ],
        ),
        out_shape=jax.ShapeDtypeStruct((DEPTH, DEC_BATCH, 2 * N_KV, N_BLK_PAST, HEAD_DIM), F32),
        compiler_params=_params(("arbitrary", "arbitrary")),
        name="compress_sample",
    )(pt, cache, pe, w1, w2)


def _cmp_select_sample_kernel(q_ref, comp_ref, gates_ref, ocmp_ref, idx_ref):
    nb = N_BLK_PAST
    t = PAST_LEN + lax.broadcasted_iota(jnp.int32, (DEC_SEQ, nb), 0)
    n = lax.broadcasted_iota(jnp.int32, (DEC_SEQ, nb), 1)
    dist = t - (n * L_BLK + (L_BLK - 1))
    seen = dist >= 0
    distf = dist.astype(F32)
    cand = n < t // L_BLK
    gates = gates_ref[0]
    n_sub = lax.broadcasted_iota(jnp.int32, (nb, nb), 0)
    m_lane = lax.broadcasted_iota(jnp.int32, (nb, nb), 1)
    k_lane = lax.broadcasted_iota(jnp.int32, (1, nb), 1)
    for g in range(N_KV):
        kc = comp_ref[0, 0, g].astype(BF16)
        vc = comp_ref[0, 0, N_KV + g].astype(BF16)
        imp = jnp.zeros((DEC_SEQ, nb), F32)
        for hh in range(HPG):
            head = g * HPG + hh
            s = _dot_nt(q_ref[0, head], kc) - _slope(head) * distf
            s = jnp.where(seen, s, NEG_INF)
            e = jnp.exp(s - jnp.max(s, axis=-1, keepdims=True))
            p = e / jnp.sum(e, axis=-1, keepdims=True)
            p = jnp.where(seen, p, 0.0)
            imp = imp + p
            o = _dot(p.astype(BF16), vc)
            ocmp_ref[0, :, head * HEAD_DIM:(head + 1) * HEAD_DIM] = o * gates[:, head:head + 1]
        score = jnp.where(cand, imp, -1.0)
        score_t = jnp.transpose(jnp.concatenate([score, jnp.zeros((SUBLANES - DEC_SEQ, nb), F32)], axis=0))
        for s_i in range(DEC_SEQ):
            rowv = score[s_i:s_i + 1, :]
            colv = score_t[:, s_i:s_i + 1]
            ahead = (rowv > colv) | ((rowv == colv) & (m_lane < n_sub))
            rank = jnp.sum(jnp.where(ahead, 1.0, 0.0), axis=1, keepdims=True).astype(jnp.int32)
            hit = rank == m_lane
            idx_k = jnp.sum(jnp.where(hit, n_sub.astype(F32), 0.0), axis=0, keepdims=True)
            val_k = jnp.sum(jnp.where(hit, colv, 0.0), axis=0, keepdims=True)
            keep = (k_lane < N_SEL - 1) & (val_k >= 0.0)
            row = g * DEC_SEQ + s_i
            idx_ref[0, row:row + 1, :] = jnp.where(keep, idx_k.astype(jnp.int32), -1)


def _cmp_select_sample(layer, q_s, comp_s, gates_s):
    return pl.pallas_call(
        _cmp_select_sample_kernel,
        grid=(DEC_BATCH,),
        in_specs=[
            pl.BlockSpec((1, N_HEADS, DEC_SEQ, HEAD_DIM), lambda b: (b, 0, 0, 0)),
            pl.BlockSpec((1, 1, 2 * N_KV, N_BLK_PAST, HEAD_DIM), lambda b: (layer, b, 0, 0, 0)),
            pl.BlockSpec((1, DEC_SEQ, LANES), lambda b: (b, 0, 0)),
        ],
        out_specs=(
            pl.BlockSpec((1, DEC_SEQ, NSA_WIDTH), lambda b: (b, 0, 0)),
            pl.BlockSpec((1, N_KV * DEC_SEQ, N_BLK_PAST), lambda b: (b, 0, 0)),
        ),
        out_shape=(
            jax.ShapeDtypeStruct((DEC_BATCH, DEC_SEQ, NSA_WIDTH), F32),
            jax.ShapeDtypeStruct((DEC_BATCH, N_KV * DEC_SEQ, N_BLK_PAST), jnp.int32),
        ),
        compiler_params=_params(("arbitrary",)),
        name="cmp_select_sample",
    )(q_s, comp_s, gates_s)


def _sel_win_sample_kernel(layer, idx_ref, pt_ref, q_ref, q2_ref, pool_ref, kvs_ref, kvw_ref, wb_ref,
                           gates_ref, ocmp_ref, out_ref, kvbuf, kwin, vwin, sem):
    b = pl.program_id(0)
    pool_base = layer * N_POOL * BLK_PER_PAGE
    n_fetch = N_SEL - 1

    def block_copy(gs, k):
        blk = jnp.maximum(idx_ref[(b * N_KV * DEC_SEQ + gs) * IDX_LANES + k], 0)
        phys = pool_base + pt_ref[b * N_PAGES + blk // BLK_PER_PAGE] * BLK_PER_PAGE + blk % BLK_PER_PAGE
        return pltpu.make_async_copy(pool_ref.at[phys], kvbuf.at[gs, k], sem)

    def for_all_blocks(fn):
        for g in range(N_KV):
            def body(i, c):
                fn(block_copy(g * DEC_SEQ + i // n_fetch, i % n_fetch))
                return c
            lax.fori_loop(0, DEC_SEQ * n_fetch, body, 0)

    for_all_blocks(lambda cp: cp.start())

    gates = gates_ref[0]
    ocmp = ocmp_ref[0]
    t_col = PAST_LEN + lax.broadcasted_iota(jnp.int32, (DEC_SEQ, 1), 0)

    o_win = []
    j = lax.broadcasted_iota(jnp.int32, (1, WIN_ROWS), 1)
    pos_w = jnp.where(j < WINDOW, PAST_LEN - WINDOW + j, PAST_LEN + j - WINDOW)
    d_w = t_col - pos_w
    ok_w = (d_w >= 0) & (d_w <= WINDOW) & (pos_w >= 0) & (j < WINDOW + DEC_SEQ)
    d_wf = d_w.astype(F32)
    pad_rows = jnp.zeros((WIN_ROWS - WINDOW - DEC_SEQ, HEAD_DIM), F32)
    for g in range(N_KV):
        ks = slice(g * HEAD_DIM, (g + 1) * HEAD_DIM)
        vs = slice(KV_WIDTH + g * HEAD_DIM, KV_WIDTH + (g + 1) * HEAD_DIM)
        kwin[0:WINDOW, :] = wb_ref[0, :, ks]
        kwin[WINDOW:WINDOW + DEC_SEQ, :] = kvw_ref[0, :, ks]
        kwin[WINDOW + DEC_SEQ:WIN_ROWS, :] = pad_rows
        vwin[0:WINDOW, :] = wb_ref[0, :, vs]
        vwin[WINDOW:WINDOW + DEC_SEQ, :] = kvw_ref[0, :, vs]
        vwin[WINDOW + DEC_SEQ:WIN_ROWS, :] = pad_rows
        kw = kwin[...].astype(BF16)
        vw = vwin[...].astype(BF16)
        for hh in range(HPG):
            head = g * HPG + hh
            s = _dot_nt(q_ref[0, head], kw) - _slope(head) * d_wf
            s = jnp.where(ok_w, s, NEG_INF)
            e = jnp.exp(s - jnp.max(s, axis=-1, keepdims=True))
            p = jnp.where(ok_w, e / jnp.sum(e, axis=-1, keepdims=True), 0.0)
            o_win.append(_dot(p.astype(BF16), vw))

    blk_zero = jnp.zeros((L_BLK, 2 * KV_WIDTH), F32)
    for gs in range(N_KV * DEC_SEQ):
        kvbuf[gs, SEL_SLOTS - 1] = blk_zero
        kvbuf[gs, SEL_SLOTS - 1, 0:DEC_SEQ, :] = kvs_ref[0]

    for_all_blocks(lambda cp: cp.wait())

    n_keys = SEL_SLOTS * L_BLK
    lane = lax.broadcasted_iota(jnp.int32, (1, n_keys), 1)
    slot_of = lane // L_BLK
    off_of = lane % L_BLK
    hrow = lax.broadcasted_iota(jnp.int32, (HPG, 1), 0)
    for g in range(N_KV):
        slope_col = jnp.zeros((HPG, 1), F32)
        for hh in range(HPG):
            slope_col = jnp.where(hrow == hh, _slope(g * HPG + hh), slope_col)
        for s_i in range(DEC_SEQ):
            gs = g * DEC_SEQ + s_i
            t_q = PAST_LEN + s_i
            base = jnp.zeros((1, n_keys), jnp.int32)
            ok = jnp.zeros((1, n_keys), jnp.int32)
            for k in range(n_fetch):
                blk = idx_ref[(b * N_KV * DEC_SEQ + gs) * IDX_LANES + k]
                here = slot_of == k
                base = jnp.where(here, blk * L_BLK, base)
                ok = jnp.where(here, jnp.where(blk >= 0, 1, 0), ok)
            cur = slot_of == SEL_SLOTS - 1
            kpos = jnp.where(cur, PAST_LEN + off_of, base + off_of)
            allowed = (cur & (off_of < DEC_SEQ) & (off_of <= s_i)) | (ok == 1)
            dist = (t_q - kpos).astype(F32)
            ks = slice(g * HEAD_DIM, (g + 1) * HEAD_DIM)
            vs = slice(KV_WIDTH + g * HEAD_DIM, KV_WIDTH + (g + 1) * HEAD_DIM)
            k_all = kvbuf[gs, :, :, ks].reshape(n_keys, HEAD_DIM).astype(BF16)
            v_all = kvbuf[gs, :, :, vs].reshape(n_keys, HEAD_DIM).astype(BF16)
            s = _dot_nt(q2_ref[0, g, s_i], k_all) - slope_col * dist
            s = jnp.where(allowed, s, NEG_INF)
            e = jnp.exp(s - jnp.max(s, axis=-1, keepdims=True))
            p = jnp.where(allowed, e / jnp.sum(e, axis=-1, keepdims=True), 0.0)
            o_sel = _dot(p.astype(BF16), v_all)
            for hh in range(HPG):
                head = g * HPG + hh
                cs = slice(head * HEAD_DIM, (head + 1) * HEAD_DIM)
                g_sel = gates[s_i:s_i + 1, N_HEADS + head:N_HEADS + head + 1]
                g_win = gates[s_i:s_i + 1, 2 * N_HEADS + head:2 * N_HEADS + head + 1]
                out_ref[0, s_i:s_i + 1, cs] = (ocmp[s_i:s_i + 1, cs] + g_sel * o_sel[hh:hh + 1, :]
                                               + g_win * o_win[head][s_i:s_i + 1, :])


def _sel_win_sample(layer, idx, page_table, q_s, q_s2, cache_sel, kvs_s, kvw_s, win_buf, gates_s, ocmp_s):
    pool = cache_sel.reshape(DEPTH * N_POOL * BLK_PER_PAGE, L_BLK, 2 * KV_WIDTH)
    wb = win_buf.reshape(DEPTH * DEC_BATCH, WINDOW, 2 * KV_WIDTH)
    idx_flat = idx[:, :, :IDX_LANES].reshape(DEC_BATCH * N_KV * DEC_SEQ * IDX_LANES)
    pt = page_table.reshape(DEC_BATCH * N_PAGES)
    b3 = lambda b, idx, pt: (b, 0, 0)
    return pl.pallas_call(
        functools.partial(_sel_win_sample_kernel, layer),
        grid_spec=pltpu.PrefetchScalarGridSpec(
            num_scalar_prefetch=2,
            grid=(DEC_BATCH,),
            in_specs=[
                pl.BlockSpec((1, N_HEADS, DEC_SEQ, HEAD_DIM), lambda b, idx, pt: (b, 0, 0, 0)),
                pl.BlockSpec((1, N_KV, DEC_SEQ, HPG, HEAD_DIM), lambda b, idx, pt: (b, 0, 0, 0, 0)),
                pl.BlockSpec(memory_space=pl.ANY),
                pl.BlockSpec((1, DEC_SEQ, 2 * KV_WIDTH), b3),
                pl.BlockSpec((1, DEC_SEQ, 2 * KV_WIDTH), b3),
                pl.BlockSpec((1, WINDOW, 2 * KV_WIDTH), lambda b, idx, pt: (layer * DEC_BATCH + b, 0, 0)),
                pl.BlockSpec((1, DEC_SEQ, LANES), b3),
                pl.BlockSpec((1, DEC_SEQ, NSA_WIDTH), b3),
            ],
            out_specs=pl.BlockSpec((1, DEC_SEQ, NSA_WIDTH), b3),
            scratch_shapes=[
                pltpu.VMEM((N_KV * DEC_SEQ, SEL_SLOTS, L_BLK, 2 * KV_WIDTH), F32),
                pltpu.VMEM((WIN_ROWS, HEAD_DIM), F32),
                pltpu.VMEM((WIN_ROWS, HEAD_DIM), F32),
                pltpu.SemaphoreType.DMA(()),
            ],
        ),
        out_shape=jax.ShapeDtypeStruct((DEC_BATCH, DEC_SEQ, NSA_WIDTH), F32),
        compiler_params=_params(("arbitrary",)),
        name="sel_win_sample",
    )(idx_flat, pt, q_s, q_s2, pool, kvs_s, kvw_s, wb, gates_s, ocmp_s)


def _prep_weights(norm1_g, w_in, gm_ln_g, gm_ln_b, w_spatial, b_spatial, out_norm_g, w_out, norm2_g,
                  w_router_group, b_router_group, w_router_expert, b_router_expert):
    gate_pad = jnp.zeros((DEPTH, D_MODEL, LANES - GATE_COLS), F32)
    w_proj = jnp.concatenate(
        [w_in[:, :, :QKV_COLS + GATE_COLS], gate_pad, w_in[:, :, QKV_COLS + GATE_COLS:]], axis=-1).astype(BF16)
    eye = jnp.eye(DEC_BATCH, dtype=F32)
    w_small = w_spatial[:, :, :DEC_SEQ, :DEC_SEQ]
    w_kron = jnp.einsum("ab,lgts->lgatbs", eye, w_small).reshape(DEPTH, GM_GROUPS, CHUNK, CHUNK)
    wmix = jnp.stack([w_spatial, w_kron], axis=1).astype(BF16)
    b_full = jnp.repeat(jnp.swapaxes(b_spatial, 1, 2), GM_DIM, axis=-1)
    b_small = jnp.tile(b_full[:, :DEC_SEQ], (1, DEC_BATCH, 1))
    bmix = jnp.stack([b_full, b_small], axis=1)
    w_r = jnp.concatenate(
        [w_router_group, w_router_expert,
         jnp.zeros((DEPTH, D_MODEL, LANES - N_EGROUPS - N_EXPERTS), F32)], axis=-1)
    wr_hi = w_r.astype(BF16)
    wr_lo = (w_r - wr_hi.astype(F32)).astype(BF16)
    b_r = jnp.concatenate(
        [b_router_group, b_router_expert, jnp.zeros((DEPTH, LANES - N_EGROUPS - N_EXPERTS), F32)], axis=-1)
    return dict(
        g1=norm1_g.reshape(DEPTH, 1, D_MODEL), w_proj=w_proj,
        lng=gm_ln_g.reshape(DEPTH, 1, GM_WIDTH), lnb=gm_ln_b.reshape(DEPTH, 1, GM_WIDTH),
        wmix=wmix, bmix=bmix, go=out_norm_g.reshape(DEPTH, 1, D_MODEL), wo=w_out.astype(BF16),
        g2=norm2_g.reshape(DEPTH, 1, D_MODEL), wr_hi=wr_hi, wr_lo=wr_lo, b_r=b_r.reshape(DEPTH, 1, LANES))


def kernel(x_prompt, x_sample, cache_cmp_kv, cache_sel_kv, state_win_kv, page_table, norm1_g, w_in, pe_cmp,
           w_cmp1, w_cmp2, gm_ln_g, gm_ln_b, w_spatial, b_spatial, out_norm_g, w_out, norm2_g,
           w_router_group, b_router_group, w_router_expert, b_router_expert, w_gate_e, w_up_e, w_down_e,
           final_norm_g):
    wts = _prep_weights(norm1_g, w_in, gm_ln_g, gm_ln_b, w_spatial, b_spatial, out_norm_g, w_out, norm2_g,
                        w_router_group, b_router_group, w_router_expert, b_router_expert)
    w1 = w_cmp1.astype(BF16)
    w2 = w_cmp2.astype(BF16)
    wg = w_gate_e.astype(BF16)
    wu = w_up_e.astype(BF16)
    wd = w_down_e.astype(BF16)
    lane_grp = np.arange(LANES) // GM_DIM
    seg = jnp.asarray(lane_grp[:, None] == lane_grp[None, :], BF16)
    tril = jnp.asarray(np.tril(np.ones((ROW_TILE, ROW_TILE), np.float32), -1), BF16)
    page_table = page_table.astype(jnp.int32)

    x_all = jnp.concatenate([
        x_prompt.reshape(N_PROMPT, D_MODEL), x_sample.reshape(N_SAMPLE, D_MODEL),
        jnp.zeros((N_TOK - N_PROMPT - N_SAMPLE, D_MODEL), F32)], axis=0)
    comp_s_all = _compress_sample(page_table, cache_cmp_kv, pe_cmp, w1, w2)

    sample_rows = slice(N_PROMPT, N_PROMPT + N_SAMPLE)
    outs = {k: [] for k in ("cmp_p", "cmp_s", "sel_p", "sel_s", "win_p", "win_s", "gm_s")}
    y_all = None
    for l in range(DEPTH):
        q_hm, kvb, kvc, kvs, kvw, gates, o_gm, vn_tail = _inproj(
            x_all, wts["g1"][l], wts["w_proj"][l], wts["lng"][l], wts["lnb"][l], seg, wts["wmix"][l],
            wts["bmix"][l])

        comp_p = _compress_prompt(kvc, pe_cmp[l], w1[l], w2[l])
        ocmp_p, sel_p = _cmp_select_prompt(q_hm, comp_p, gates)
        o_nsa_p = _sel_win_prompt(q_hm, kvb, sel_p, gates, ocmp_p)

        q_s = q_hm[:, sample_rows].reshape(N_HEADS, DEC_BATCH, DEC_SEQ, HEAD_DIM)
        q_s1 = jnp.transpose(q_s, (1, 0, 2, 3))
        q_s2 = jnp.transpose(q_s.reshape(N_KV, HPG, DEC_BATCH, DEC_SEQ, HEAD_DIM), (2, 0, 3, 1, 4))
        gates_s = gates[sample_rows].reshape(DEC_BATCH, DEC_SEQ, LANES)
        kvs_s = kvs[sample_rows].reshape(DEC_BATCH, DEC_SEQ, 2 * KV_WIDTH)
        kvw_s = kvw[sample_rows].reshape(DEC_BATCH, DEC_SEQ, 2 * KV_WIDTH)
        ocmp_s, idx_s = _cmp_select_sample(l, q_s1, comp_s_all, gates_s)
        o_nsa_s = _sel_win_sample(l, idx_s, page_table, q_s1, q_s2, cache_sel_kv, kvs_s, kvw_s, state_win_kv,
                                  gates_s, ocmp_s)
        o_nsa_s = jnp.concatenate([o_nsa_s.reshape(N_SAMPLE, NSA_WIDTH),
                                   jnp.zeros((ROW_TILE - N_SAMPLE, NSA_WIDTH), F32)], axis=0)

        x_mid, packed, meta, counts = _mixout(
            x_all, o_nsa_p, o_nsa_s, o_gm, wts["go"][l], wts["wo"][l], wts["g2"][l], wts["wr_hi"][l],
            wts["wr_lo"][l], wts["b_r"][l], tril)
        pos, items = _moe_plan(meta, counts)
        xs = _scatter_rows(pos, packed)
        ys = _expert_mlp(items, xs, wg[l], wu[l], wd[l])
        if l == DEPTH - 1:
            x_all, y_all = _unsort_residual(pos, x_mid, ys, final_norm_g.reshape(1, D_MODEL))
        else:
            (x_all,) = _unsort_residual(pos, x_mid, ys)

        kv_shape = (2, N_KV, HEAD_DIM)
        outs["cmp_p"].append(kvc[:N_PROMPT].reshape(BATCH, SEQ, *kv_shape))
        outs["cmp_s"].append(kvc[sample_rows].reshape(DEC_BATCH, DEC_SEQ, *kv_shape))
        outs["sel_p"].append(kvs[:N_PROMPT].reshape(BATCH, SEQ, *kv_shape))
        outs["sel_s"].append(kvs[sample_rows].reshape(DEC_BATCH, DEC_SEQ, *kv_shape))
        outs["win_p"].append(kvw[:N_PROMPT].reshape(BATCH, SEQ, *kv_shape)[:, SEQ - WINDOW:])
        outs["win_s"].append(kvw[sample_rows].reshape(DEC_BATCH, DEC_SEQ, *kv_shape))
        outs["gm_s"].append(vn_tail[:N_SAMPLE].reshape(DEC_BATCH, DEC_SEQ, GM_GROUPS, GM_DIM))

    y_prompt = y_all[:N_PROMPT].reshape(BATCH, SEQ, D_MODEL)
    y_sample = y_all[sample_rows].reshape(DEC_BATCH, DEC_SEQ, D_MODEL)
    return (y_prompt, y_sample, jnp.stack(outs["cmp_p"]), jnp.stack(outs["cmp_s"]), jnp.stack(outs["sel_p"]),
            jnp.stack(outs["sel_s"]), jnp.stack(outs["win_p"]), jnp.stack(outs["win_s"]), jnp.stack(outs["gm_s"]))
```
